```python
import jax, jax.numpy as jnp
from jax import lax
import numpy as np

D_MODEL = 1024
BATCH = 16
SEQ = 2048
DEPTH = 2
DEC_BATCH = 32
DEC_SEQ = 8
PAST_LEN = 16384
PAGE_SIZE = 128

N_A_LAYERS = (DEPTH + 1) // 2
N_C_LAYERS = DEPTH // 2

POOL_WINDOWS = (2, 4, 8, 16)
POOL_GROUPS = len(POOL_WINDOWS)
POOL_WIDTH = D_MODEL // 2
POOL_GW = POOL_WIDTH // POOL_GROUPS
POOL_BUF = max(POOL_WINDOWS) - 1
MLSTM_HEADS = 4
MLSTM_WIDTH = D_MODEL // 2
MLSTM_HD = MLSTM_WIDTH // MLSTM_HEADS
MLSTM_CHUNK = 64
CONV_W = 4
MIX_A = POOL_WIDTH + MLSTM_WIDTH
IN_A = 2 * POOL_WIDTH + 5 * MLSTM_WIDTH + 2 * MLSTM_HEADS
NSA_HEADS = 16
NSA_HD = D_MODEL // NSA_HEADS
NSA_KV = 4
NSA_REP = NSA_HEADS // NSA_KV
NSA_WIDTH = NSA_HEADS * NSA_HD
KVW = NSA_KV * NSA_HD
IN_C = NSA_WIDTH + 6 * KVW + 3 * NSA_HEADS + NSA_WIDTH
CMP_STRIDE = 16
CMP_BLOCK = 2 * CMP_STRIDE
SEL_BLOCK = 64
SEL_TOP = 16
WINDOW = 512
ROPE_THETA = 500000.0
ROPE_DIM = NSA_HD // 4
Q_BLOCK = 16
NEG = -1e30
FORCE = 1e4
EPS = 1e-6

kernel_name = "hybrid_pool_mlstm_nsa_decode_step"


def rmsnorm(x, g):
    xf = x.astype(jnp.float32)
    y = xf * lax.rsqrt(jnp.mean(xf * xf, axis=-1, keepdims=True) + EPS)
    return (y * g.astype(jnp.float32)).astype(x.dtype)


def ada_mod(c, w, b):
    m = jnp.einsum('bd,de->be', jax.nn.silu(c), w) + b
    shift, scale, gate = jnp.split(m, 3, axis=-1)
    return shift[:, None], scale[:, None], gate[:, None]


def rope_partial(x, pos):
    half = ROPE_DIM // 2
    freq = jnp.power(ROPE_THETA, -2.0 * jnp.arange(half, dtype=jnp.float32) / ROPE_DIM)
    ang = pos.astype(jnp.float32)[:, None] * freq[None, :]
    shape = (1, ang.shape[0]) + (1,) * (x.ndim - 3) + (half,)
    cos, sin = jnp.cos(ang).reshape(shape), jnp.sin(ang).reshape(shape)
    xf = x.astype(jnp.float32)
    x1, x2 = xf[..., :half], xf[..., half:ROPE_DIM]
    out = jnp.concatenate([x1 * cos - x2 * sin, x2 * cos + x1 * sin, xf[..., ROPE_DIM:]], axis=-1)
    return out.astype(x.dtype)


def masked_softmax(s, mask):
    s = jnp.where(mask, s.astype(jnp.float32), NEG)
    m = jnp.max(s, axis=-1, keepdims=True)
    p = jnp.where(mask, jnp.exp(s - m), 0.0)
    return p / jnp.maximum(jnp.sum(p, axis=-1, keepdims=True), 1e-30)


def pool_mix(xp, buf, pos0, w_grp, scale):
    B, T, _ = xp.shape
    xc = jnp.concatenate([buf.astype(xp.dtype), xp], axis=1)
    xf = xc.astype(jnp.float32)
    cs = jnp.concatenate([jnp.zeros_like(xf[:, :1]), jnp.cumsum(xf, axis=1)], axis=1)
    end = cs[:, POOL_BUF + 1:]
    pos = pos0 + jnp.arange(T)
    means = []
    for g, w in enumerate(POOL_WINDOWS):
        sl = slice(g * POOL_GW, (g + 1) * POOL_GW)
        start = cs[:, POOL_BUF + 1 - w:POOL_BUF + 1 - w + T, sl]
        cnt = jnp.minimum(pos + 1, w).astype(jnp.float32)[None, :, None]
        means.append((end[..., sl] - start) / cnt)
    d = (jnp.concatenate(means, axis=-1) - xp.astype(jnp.float32)).reshape(B, T, POOL_GROUPS, POOL_GW)
    y = jnp.einsum('btgi,gio->btgo', d, w_grp.astype(jnp.float32)).reshape(B, T, POOL_WIDTH) * scale.astype(jnp.float32)
    return y.astype(xp.dtype), xc[:, T:]


def causal_conv(x, buf, w, b):
    T = x.shape[1]
    xc = jnp.concatenate([buf.astype(x.dtype), x], axis=1)
    y = b + sum(xc[:, j:j + T] * w[j] for j in range(CONV_W))
    return y, xc[:, T:]


def mlstm_chunkwise(q, k, v, ig, lf, C0, n0, m0):
    B, H, T, _ = q.shape
    L = min(MLSTM_CHUNK, T)
    nc = T // L

    def chunks(a):
        return jnp.moveaxis(a.reshape((B, H, nc, L) + a.shape[3:]), 2, 0)

    causal = jnp.tril(jnp.ones((L, L), bool))

    def step(carry, xs):
        C, n, m = carry
        qc, kc, vc, ic, fc = xs
        b = jnp.cumsum(fc, axis=-1)
        a_inter = b + m[..., None]
        Dm = jnp.where(causal, b[..., :, None] - b[..., None, :] + ic[..., None, :], -jnp.inf)
        mt = jnp.maximum(a_inter, jnp.max(Dm, axis=-1))
        w_inter = jnp.exp(a_inter - mt)
        s = jnp.einsum('bhtd,bhsd->bhts', qc, kc) * jnp.exp(Dm - mt[..., None])
        num = w_inter[..., None] * jnp.einsum('bhtd,bhde->bhte', qc, C) + jnp.einsum('bhts,bhse->bhte', s, vc)
        den = w_inter * jnp.einsum('bhtd,bhd->bht', qc, n) + jnp.sum(s, axis=-1)
        h = num / jnp.maximum(jnp.abs(den), jnp.exp(-mt))[..., None]
        bl = b[..., -1]
        g = bl[..., None] - b + ic
        m_new = jnp.maximum(bl + m, jnp.max(g, axis=-1))
        decay = jnp.exp(bl + m - m_new)
        wg = jnp.exp(g - m_new[..., None])
        C_new = decay[..., None, None] * C + jnp.einsum('bhs,bhsd,bhse->bhde', wg, kc, vc)
        n_new = decay[..., None] * n + jnp.einsum('bhs,bhsd->bhd', wg, kc)
        return (C_new, n_new, m_new), h

    (C, n, m), hs = lax.scan(step, (C0, n0, m0), (chunks(q), chunks(k), chunks(v), chunks(ig), chunks(lf)))
    return jnp.moveaxis(hs, 0, 2).reshape(B, H, T, -1), C, n, m


def layer_a(x, c, pos0, pool_buf, conv_buf, C0, n0, m0,
            norm_w, ada_w, ada_b, w_in, b_if, pool_w, pool_scale, conv_w, conv_b, head_norm, w_out):
    B, T, _ = x.shape
    shift, scale, gate = ada_mod(c, ada_w, ada_b)
    h = rmsnorm(x, norm_w) * (1 + scale) + shift
    u = jnp.einsum('btd,de->bte', h, w_in)
    P, M, H = POOL_WIDTH, MLSTM_WIDTH, MLSTM_HEADS
    xpool, zpool, qk, v, o, z, gif = jnp.split(u, [P, 2 * P, 2 * P + 2 * M, 2 * P + 3 * M, 2 * P + 4 * M, 2 * P + 5 * M], axis=-1)
    y_pool, new_pool = pool_mix(xpool, pool_buf, pos0, pool_w, pool_scale)
    y_pool = y_pool * jax.nn.silu(zpool)
    qk, new_conv = causal_conv(qk, conv_buf, conv_w, conv_b)
    q, k = jnp.split(jax.nn.silu(qk), 2, axis=-1)

    def heads(a):
        return a.astype(jnp.float32).reshape(B, T, H, MLSTM_HD).transpose(0, 2, 1, 3)

    gif = gif.astype(jnp.float32) + b_if.astype(jnp.float32)
    ig = gif[..., :H].transpose(0, 2, 1)
    lf = jax.nn.log_sigmoid(gif[..., H:]).transpose(0, 2, 1)
    hm, C, n, m = mlstm_chunkwise(heads(q), heads(k) * (MLSTM_HD ** -0.5), heads(v), ig, lf,
                                  C0.astype(jnp.float32), n0.astype(jnp.float32), m0.astype(jnp.float32))
    hm = hm.transpose(0, 2, 1, 3)
    hm = hm * lax.rsqrt(jnp.mean(hm * hm, axis=-1, keepdims=True) + EPS)
    hm = hm.reshape(B, T, M) * head_norm.astype(jnp.float32)
    y_m = (hm * jax.nn.sigmoid(o.astype(jnp.float32)) * jax.nn.silu(z.astype(jnp.float32))).astype(x.dtype)
    y = jnp.einsum('bte,ed->btd', jnp.concatenate([y_pool, y_m], axis=-1), w_out)
    return x + gate * y, new_pool, new_conv, C, n, m


def compress(rows, w_pos, w1, b1, w2):
    B, S = rows.shape[:2]
    n_sub = S // CMP_STRIDE
    r = rows[:, :n_sub * CMP_STRIDE].reshape(B, n_sub, CMP_STRIDE, NSA_KV, NSA_HD)
    lo = jnp.einsum('bnigd,i->bngd', r, w_pos[:CMP_STRIDE])
    hi = jnp.einsum('bnigd,i->bngd', r, w_pos[CMP_STRIDE:])
    pooled = lo[:, :-1] + hi[:, 1:]
    hid = jax.nn.gelu(jnp.einsum('bngd,de->bnge', pooled, w1) + b1)
    return jnp.einsum('bnge,ef->bngf', hid, w2)


def compressed_kv(k_rows, v_rows, kpos, kw1, kb1, kw2, vpos, vw1, vb1, vw2):
    kc = compress(k_rows, kpos, kw1, kb1, kw2)
    vc = compress(v_rows, vpos, vw1, vb1, vw2)
    c_end = jnp.arange(kc.shape[1]) * CMP_STRIDE + CMP_BLOCK - 1
    return rope_partial(kc, c_end), vc, c_end


def cmp_to_sel(n_cmp, n_sel):
    cs = np.arange(n_cmp)[:, None] * CMP_STRIDE
    ss = np.arange(n_sel)[None, :] * SEL_BLOCK
    return jnp.asarray((cs < ss + SEL_BLOCK) & (cs + CMP_BLOCK > ss), jnp.float32)


def nsa_attend(q, qpos, gates, kc, vc, c_end, overlap, fetch_sel, kw, vw, kw_pos):
    sc = NSA_HD ** -0.5
    Bq, Tq, G, R = q.shape[:4]
    mask_c = (c_end[None, :] <= qpos[:, None])[None, :, None, None, :]
    p_c = masked_softmax(jnp.einsum('btgrd,bngd->btgrn', q, kc) * sc, mask_c)
    o_c = jnp.einsum('btgrn,bngd->btgrd', p_c, vc.astype(jnp.float32))
    imp = jnp.einsum('btgn,ns->btgs', p_c.sum(3), overlap)
    n_sel = overlap.shape[1]
    blk = jnp.arange(n_sel)[None, :]
    cur = (qpos // SEL_BLOCK)[:, None]
    valid = (blk <= cur)[None, :, None, :]
    forced = ((blk == 0) | (blk == cur) | (blk == cur - 1))[None, :, None, :]
    score = jnp.where(valid, imp + jnp.where(forced, FORCE, 0.0), NEG)
    top_val, top_idx = lax.top_k(score, min(SEL_TOP, n_sel))
    ok = top_val > 0.5 * NEG
    K = top_idx.shape[-1]
    ks, vs = fetch_sel(top_idx)
    kpos = top_idx[..., None] * SEL_BLOCK + jnp.arange(SEL_BLOCK)
    mask_s = ok[..., None] & (kpos <= qpos[None, :, None, None, None])
    s_s = jnp.einsum('btgrd,btgkjd->btgrkj', q, ks) * sc
    p_s = masked_softmax(s_s.reshape(Bq, Tq, G, R, K * SEL_BLOCK), mask_s.reshape(Bq, Tq, G, 1, K * SEL_BLOCK))
    o_s = jnp.einsum('btgrm,btgmd->btgrd', p_s, vs.reshape(Bq, Tq, G, K * SEL_BLOCK, NSA_HD).astype(jnp.float32))
    kp, qp = kw_pos[None, :], qpos[:, None]
    mask_w = ((kp <= qp) & (kp > qp - WINDOW) & (kp >= 0))[None, :, None, None, :]
    p_w = masked_softmax(jnp.einsum('btgrd,bsgd->btgrs', q, kw) * sc, mask_w)
    o_w = jnp.einsum('btgrs,bsgd->btgrd', p_w, vw.astype(jnp.float32))
    g = gates.astype(jnp.float32)
    return g[..., 0, None] * o_c + g[..., 1, None] * o_s + g[..., 2, None] * o_w


def nsa_inputs(x, c, pos, norm_w, ada_w, ada_b, w_in, b_gate):
    B, T, _ = x.shape
    shift, scale, gate = ada_mod(c, ada_w, ada_b)
    h = rmsnorm(x, norm_w) * (1 + scale) + shift
    u = jnp.einsum('btd,de->bte', h, w_in)
    cuts = np.cumsum([NSA_WIDTH] + [KVW] * 6 + [3 * NSA_HEADS]).tolist()
    q, kc, vc, ks, vs, kw, vw, g, z = jnp.split(u, cuts, axis=-1)

    def kv(a):
        return a.reshape(B, T, NSA_KV, NSA_HD)

    q = rope_partial(q.reshape(B, T, NSA_KV, NSA_REP, NSA_HD), pos)
    gates = jax.nn.sigmoid((g + b_gate).reshape(B, T, NSA_KV, NSA_REP, 3))
    return q, kv(kc), kv(vc), rope_partial(kv(ks), pos), kv(vs), rope_partial(kv(kw), pos), kv(vw), gates, z, gate


def nsa_output(x, o, z, gate, w_out):
    B, T = x.shape[:2]
    y = (o.reshape(B, T, NSA_WIDTH) * jax.nn.silu(z.astype(jnp.float32))).astype(x.dtype)
    return x + gate * jnp.einsum('bte,ed->btd', y, w_out)


def layer_c_prompt(x, c, norm_w, ada_w, ada_b, w_in, b_gate,
                   kpos, kw1, kb1, kw2, vpos, vw1, vb1, vw2, w_out):
    B, T, _ = x.shape
    pos = jnp.arange(T)
    q, kcr, vcr, ks, vs, kw, vw, gates, z, gate = nsa_inputs(x, c, pos, norm_w, ada_w, ada_b, w_in, b_gate)
    kc, vc, c_end = compressed_kv(kcr, vcr, kpos, kw1, kb1, kw2, vpos, vw1, vb1, vw2)
    n_sel = T // SEL_BLOCK
    overlap = cmp_to_sel(kc.shape[1], n_sel)
    ks_b = ks.reshape(B, n_sel, SEL_BLOCK, NSA_KV, NSA_HD)
    vs_b = vs.reshape(B, n_sel, SEL_BLOCK, NSA_KV, NSA_HD)
    b_idx = jnp.arange(B)[:, None, None, None]
    g_idx = jnp.arange(NSA_KV)[None, None, :, None]

    def fetch(idx):
        return ks_b[b_idx, idx, :, g_idx], vs_b[b_idx, idx, :, g_idx]

    pad = jnp.zeros((B, WINDOW, NSA_KV, NSA_HD), kw.dtype)
    kw_p = jnp.concatenate([pad, kw], axis=1)
    vw_p = jnp.concatenate([pad, vw], axis=1)

    def block(i):
        s = i * Q_BLOCK
        qb = lax.dynamic_slice_in_dim(q, s, Q_BLOCK, axis=1)
        gb = lax.dynamic_slice_in_dim(gates, s, Q_BLOCK, axis=1)
        kwb = lax.dynamic_slice_in_dim(kw_p, s, Q_BLOCK + WINDOW, axis=1)
        vwb = lax.dynamic_slice_in_dim(vw_p, s, Q_BLOCK + WINDOW, axis=1)
        qpos = s + jnp.arange(Q_BLOCK)
        kwpos = s - WINDOW + jnp.arange(Q_BLOCK + WINDOW)
        return nsa_attend(qb, qpos, gb, kc, vc, c_end, overlap, fetch, kwb, vwb, kwpos)

    o = lax.map(block, jnp.arange(T // Q_BLOCK))
    o = jnp.moveaxis(o, 0, 1).reshape(B, T, NSA_WIDTH)
    wb = min(WINDOW, T)
    return nsa_output(x, o, z, gate, w_out), kcr, vcr, ks, vs, kw[:, T - wb:], vw[:, T - wb:]


def layer_c_sample(x, c, page_table, pool_cmp_k, pool_cmp_v, pool_sel_k, pool_sel_v, win_k, win_v,
                   norm_w, ada_w, ada_b, w_in, b_gate,
                   kpos, kw1, kb1, kw2, vpos, vw1, vb1, vw2, w_out):
    B, T, _ = x.shape
    n_pages = page_table.shape[1]
    page = pool_cmp_k.shape[1]
    past = n_pages * page
    pos = past + jnp.arange(T)
    q, kcr, vcr, ks, vs, kw, vw, gates, z, gate = nsa_inputs(x, c, pos, norm_w, ada_w, ada_b, w_in, b_gate)

    def dense(pool, new):
        return jnp.concatenate([pool[page_table].reshape(B, past, NSA_KV, NSA_HD).astype(new.dtype), new], axis=1)

    kc, vc, c_end = compressed_kv(dense(pool_cmp_k, kcr), dense(pool_cmp_v, vcr), kpos, kw1, kb1, kw2, vpos, vw1, vb1, vw2)
    S = past + T
    n_sel = -(-S // SEL_BLOCK)
    overlap = cmp_to_sel(kc.shape[1], n_sel)
    n_past_blk = past // SEL_BLOCK
    n_new_blk = n_sel - n_past_blk
    bpp = page // SEL_BLOCK

    def new_blocks(a):
        a = jnp.pad(a, ((0, 0), (0, n_new_blk * SEL_BLOCK - T), (0, 0), (0, 0)))
        return a.reshape(B, n_new_blk, SEL_BLOCK, NSA_KV, NSA_HD)

    ks_new, vs_new = new_blocks(ks), new_blocks(vs)
    pk = pool_sel_k.reshape(-1, bpp, SEL_BLOCK, NSA_KV, NSA_HD)
    pv = pool_sel_v.reshape(-1, bpp, SEL_BLOCK, NSA_KV, NSA_HD)
    b_idx = jnp.arange(B)[:, None, None, None]
    g_idx = jnp.arange(NSA_KV)[None, None, :, None]

    def fetch(idx):
        in_past = (idx < n_past_blk)[..., None, None]
        pg = page_table[b_idx, jnp.minimum(idx // bpp, n_pages - 1)]
        sub = idx % bpp
        j = jnp.clip(idx - n_past_blk, 0, n_new_blk - 1)

        def get(pool, new):
            return jnp.where(in_past, pool[pg, sub, :, g_idx].astype(new.dtype), new[b_idx, j, :, g_idx])

        return get(pk, ks_new), get(pv, vs_new)

    wb = win_k.shape[1]
    kw_all = jnp.concatenate([win_k.astype(kw.dtype), kw], axis=1)
    vw_all = jnp.concatenate([win_v.astype(vw.dtype), vw], axis=1)
    kw_pos = past - wb + jnp.arange(wb + T)
    o = nsa_attend(q, pos, gates, kc, vc, c_end, overlap, fetch, kw_all, vw_all, kw_pos)
    y = nsa_output(x, o.reshape(B, T, NSA_WIDTH), z, gate, w_out)
    return y, kcr, vcr, ks, vs, kw_all[:, T:], vw_all[:, T:]


def setup_inputs(seed: int = 0) -> dict:
    key = jax.random.key(seed)
    keys = iter(jax.random.split(key, 64))

    def nrm(shape, std=1.0):
        return jax.random.normal(next(keys), shape, jnp.float32) * std

    NA, NC, D = N_A_LAYERS, N_C_LAYERS, D_MODEL
    n_pages = PAST_LEN // PAGE_SIZE
    used = DEC_BATCH * n_pages
    pool_pages = used + max(1, used // 4)
    win_buf = min(WINDOW, PAST_LEN)
    page_table = jax.random.permutation(next(keys), pool_pages)[:used].reshape(DEC_BATCH, n_pages).astype(jnp.int32)
    kv_pool = (NC, pool_pages, PAGE_SIZE, NSA_KV, NSA_HD)
    f_bias = jnp.linspace(3.0, 6.0, MLSTM_HEADS, dtype=jnp.float32)
    return {
        "x_prompt": nrm((BATCH, SEQ, D)),
        "x_sample": nrm((DEC_BATCH, DEC_SEQ, D)),
        "c_prompt": nrm((BATCH, D)),
        "c_sample": nrm((DEC_BATCH, D)),
        "state_pool": nrm((NA, DEC_BATCH, POOL_BUF, POOL_WIDTH)),
        "state_conv": nrm((NA, DEC_BATCH, CONV_W - 1, 2 * MLSTM_WIDTH)),
        "state_C": nrm((NA, DEC_BATCH, MLSTM_HEADS, MLSTM_HD, MLSTM_HD), 0.1),
        "state_n": nrm((NA, DEC_BATCH, MLSTM_HEADS, MLSTM_HD), 0.1),
        "state_m": nrm((NA, DEC_BATCH, MLSTM_HEADS), 0.5),
        "cache_cmp_k": nrm(kv_pool),
        "cache_cmp_v": nrm(kv_pool),
        "cache_sel_k": nrm(kv_pool),
        "cache_sel_v": nrm(kv_pool),
        "cache_win_k": nrm((NC, DEC_BATCH, win_buf, NSA_KV, NSA_HD)),
        "cache_win_v": nrm((NC, DEC_BATCH, win_buf, NSA_KV, NSA_HD)),
        "page_table": page_table,
        "a_norm": 1.0 + nrm((NA, D), 0.02),
        "a_ada_w": nrm((NA, D, 3 * D), 0.5 * D ** -0.5),
        "a_ada_b": nrm((NA, 3 * D), 0.02),
        "a_w_in": nrm((NA, D, IN_A), D ** -0.5),
        "a_b_if": jnp.concatenate([nrm((NA, MLSTM_HEADS), 0.1), f_bias + nrm((NA, MLSTM_HEADS), 0.1)], axis=-1),
        "a_pool_w": nrm((NA, POOL_GROUPS, POOL_GW, POOL_GW), POOL_GW ** -0.5),
        "a_pool_scale": 1.0 + nrm((NA, POOL_WIDTH), 0.1),
        "a_conv_w": nrm((NA, CONV_W, 2 * MLSTM_WIDTH), CONV_W ** -0.5),
        "a_conv_b": nrm((NA, 2 * MLSTM_WIDTH), 0.02),
        "a_head_norm": 1.0 + nrm((NA, MLSTM_WIDTH), 0.02),
        "a_w_out": nrm((NA, MIX_A, D), MIX_A ** -0.5),
        "c_norm": 1.0 + nrm((NC, D), 0.02),
        "c_ada_w": nrm((NC, D, 3 * D), 0.5 * D ** -0.5),
        "c_ada_b": nrm((NC, 3 * D), 0.02),
        "c_w_in": nrm((NC, D, IN_C), D ** -0.5),
        "c_b_gate": nrm((NC, 3 * NSA_HEADS), 0.1),
        "c_cmp_k_pos": CMP_BLOCK ** -0.5 * (1.0 + nrm((NC, CMP_BLOCK), 0.1)),
        "c_cmp_k_w1": nrm((NC, NSA_HD, NSA_HD), NSA_HD ** -0.5),
        "c_cmp_k_b1": nrm((NC, NSA_HD), 0.02),
        "c_cmp_k_w2": nrm((NC, NSA_HD, NSA_HD), 2.0 * NSA_HD ** -0.5),
        "c_cmp_v_pos": CMP_BLOCK ** -0.5 * (1.0 + nrm((NC, CMP_BLOCK), 0.1)),
        "c_cmp_v_w1": nrm((NC, NSA_HD, NSA_HD), NSA_HD ** -0.5),
        "c_cmp_v_b1": nrm((NC, NSA_HD), 0.02),
        "c_cmp_v_w2": nrm((NC, NSA_HD, NSA_HD), 2.0 * NSA_HD ** -0.5),
        "c_w_out": nrm((NC, NSA_WIDTH, D), NSA_WIDTH ** -0.5),
        "final_norm": 1.0 + nrm((D,), 0.02),
    }


def reference(x_prompt, x_sample, c_prompt, c_sample,
              state_pool, state_conv, state_C, state_n, state_m,
              cache_cmp_k, cache_cmp_v, cache_sel_k, cache_sel_v, cache_win_k, cache_win_v,
              page_table,
              a_norm, a_ada_w, a_ada_b, a_w_in, a_b_if, a_pool_w, a_pool_scale, a_conv_w, a_conv_b, a_head_norm, a_w_out,
              c_norm, c_ada_w, c_ada_b, c_w_in, c_b_gate,
              c_cmp_k_pos, c_cmp_k_w1, c_cmp_k_b1, c_cmp_k_w2, c_cmp_v_pos, c_cmp_v_w1, c_cmp_v_b1, c_cmp_v_w2,
              c_w_out, final_norm):
    Bp = x_prompt.shape[0]
    past = page_table.shape[1] * cache_cmp_k.shape[2]
    xp, xs = x_prompt, x_sample
    pool_p, pool_s, conv_p, conv_s, C_p, C_s, n_p, n_s, m_p, m_s = ([] for _ in range(10))
    ck_p, ck_s, cv_p, cv_s, sk_p, sk_s, sv_p, sv_s, wk_p, wk_s, wv_p, wv_s = ([] for _ in range(12))
    for layer in range(DEPTH):
        i = layer // 2
        if layer % 2 == 0:
            w = (a_norm[i], a_ada_w[i], a_ada_b[i], a_w_in[i], a_b_if[i], a_pool_w[i], a_pool_scale[i],
                 a_conv_w[i], a_conv_b[i], a_head_norm[i], a_w_out[i])
            xp, pb, cb, C, n, m = layer_a(
                xp, c_prompt, 0,
                jnp.zeros((Bp, POOL_BUF, POOL_WIDTH), xp.dtype),
                jnp.zeros((Bp, CONV_W - 1, 2 * MLSTM_WIDTH), xp.dtype),
                jnp.zeros((Bp, MLSTM_HEADS, MLSTM_HD, MLSTM_HD), jnp.float32),
                jnp.zeros((Bp, MLSTM_HEADS, MLSTM_HD), jnp.float32),
                jnp.zeros((Bp, MLSTM_HEADS), jnp.float32), *w)
            pool_p.append(pb); conv_p.append(cb); C_p.append(C); n_p.append(n); m_p.append(m)
            xs, pb, cb, C, n, m = layer_a(xs, c_sample, past, state_pool[i], state_conv[i],
                                          state_C[i], state_n[i], state_m[i], *w)
            pool_s.append(pb); conv_s.append(cb); C_s.append(C); n_s.append(n); m_s.append(m)
        else:
            w = (c_norm[i], c_ada_w[i], c_ada_b[i], c_w_in[i], c_b_gate[i],
                 c_cmp_k_pos[i], c_cmp_k_w1[i], c_cmp_k_b1[i], c_cmp_k_w2[i],
                 c_cmp_v_pos[i], c_cmp_v_w1[i], c_cmp_v_b1[i], c_cmp_v_w2[i], c_w_out[i])
            xp, a1, a2, a3, a4, a5, a6 = layer_c_prompt(xp, c_prompt, *w)
            ck_p.append(a1); cv_p.append(a2); sk_p.append(a3); sv_p.append(a4); wk_p.append(a5); wv_p.append(a6)
            xs, a1, a2, a3, a4, a5, a6 = layer_c_sample(xs, c_sample, page_table, cache_cmp_k[i], cache_cmp_v[i],
                                                        cache_sel_k[i], cache_sel_v[i], cache_win_k[i], cache_win_v[i], *w)
            ck_s.append(a1); cv_s.append(a2); sk_s.append(a3); sv_s.append(a4); wk_s.append(a5); wv_s.append(a6)
    y_prompt = rmsnorm(xp, final_norm)
    y_sample = rmsnorm(xs, final_norm)
    return (y_prompt, y_sample,
            jnp.stack(pool_p), jnp.stack(pool_s), jnp.stack(conv_p), jnp.stack(conv_s),
            jnp.stack(C_p), jnp.stack(C_s), jnp.stack(n_p), jnp.stack(n_s), jnp.stack(m_p), jnp.stack(m_s),
            jnp.stack(ck_p), jnp.stack(ck_s), jnp.stack(cv_p), jnp.stack(cv_s),
            jnp.stack(sk_p), jnp.stack(sk_s), jnp.stack(sv_p), jnp.stack(sv_s),
            jnp.stack(wk_p), jnp.stack(wk_s), jnp.stack(wv_p), jnp.stack(wv_s))
```

```python
import functools

import numpy as np
import jax
import jax.numpy as jnp
from jax import lax
from jax.experimental import pallas as pl
from jax.experimental.pallas import tpu as pltpu

F32, BF16, I32 = jnp.float32, jnp.bfloat16, jnp.int32

POOL_WINDOWS = (2, 4, 8, 16)
MLSTM_CHUNK = 64
CMP_STRIDE = 16
CMP_BLOCK = 2 * CMP_STRIDE
SEL_BLOCK = 64
SEL_TOP = 16
WINDOW = 512
ROPE_THETA = 500000.0
ROPE_DIM = 16
NEG = -1e30
FORCE = 1e4
EPS = 1e-6

LANES = 128
HALF = LANES // 2
TQ = 128


def _cp(sem=None, vmem_mb=48):
    return pltpu.CompilerParams(dimension_semantics=sem, vmem_limit_bytes=vmem_mb << 20)


def _silu(x):
    return x / (1.0 + jnp.exp(-x))


def _sigmoid(x):
    return 1.0 / (1.0 + jnp.exp(-x))


def _nt(a, b):
    return lax.dot_general(a, b, (((1,), (1,)), ((), ())), preferred_element_type=F32)


def _tn(a, b):
    return lax.dot_general(a, b, (((0,), (0,)), ((), ())), preferred_element_type=F32)


def _dot(a, b):
    return jnp.dot(a, b, preferred_element_type=F32)


def _split_dot(a, b_bf16):
    hi = a.astype(BF16)
    lo = (a - hi.astype(F32)).astype(BF16)
    return _dot(hi, b_bf16) + _dot(lo, b_bf16)


def _rms(x, w):
    return x * lax.rsqrt(jnp.mean(x * x, axis=-1, keepdims=True) + EPS) * w


def _rope_tile(c, cs, s1, s2):
    return c * cs + pltpu.roll(c, LANES - ROPE_DIM // 2, 1) * s1 + pltpu.roll(c, ROPE_DIM // 2, 1) * s2


def _half_of(shape):
    return lax.broadcasted_iota(I32, shape, len(shape) - 1) >= HALF


def _to_half(x, src_half, dst_half):
    hi = _half_of(x.shape)
    y = jnp.where(hi == src_half, x, 0.0)
    both = y + pltpu.roll(y, HALF, 1)
    return jnp.where(hi == dst_half, both, 0.0)


def _masked_softmax_parts(s, mask):
    s = jnp.where(mask, s, NEG)
    m = jnp.max(s, axis=-1, keepdims=True)
    p = jnp.where(mask, jnp.exp(s - m), 0.0)
    return p, jnp.sum(p, axis=-1, keepdims=True)


def _ada_body(c_ref, w_ref, b_ref, o_ref):
    a = _silu(c_ref[...])
    o_ref[...] = jnp.dot(a, w_ref[...], preferred_element_type=F32, precision=lax.Precision.HIGHEST) + b_ref[...]


def ada_mod(c, w, b):
    bc, d = c.shape
    n = w.shape[1]
    tn = 512
    m = pl.pallas_call(
        _ada_body,
        out_shape=jax.ShapeDtypeStruct((bc, n), F32),
        grid=(n // tn,),
        in_specs=[pl.BlockSpec((bc, d), lambda j: (0, 0)), pl.BlockSpec((d, tn), lambda j: (0, j)),
                  pl.BlockSpec((1, tn), lambda j: (0, j))],
        out_specs=pl.BlockSpec((bc, tn), lambda j: (0, j)),
        compiler_params=_cp(("arbitrary",)),
        name="ada_mod",
    )(c, w, b.reshape(1, n))
    return m[:, :d], m[:, d:2 * d], m[:, 2 * d:]


def _mod_rows(v, bsz, t, tm):
    d = v.shape[-1]
    if t % tm == 0:
        return v.reshape(bsz, 1, d), t // tm
    assert tm == bsz * t
    return jnp.broadcast_to(v[:, None, :], (bsz, t, d)).reshape(1, tm, d), 1


def _norm_mod(x_ref, nw_ref, sc_ref, sh_ref):
    return (_rms(x_ref[...], nw_ref[...]) * (1.0 + sc_ref[0]) + sh_ref[0]).astype(BF16)


def _proj_a_body(x_ref, sh_ref, sc_ref, nw_ref, w_ref, *outs, segs):
    h = _norm_mod(x_ref, nw_ref, sc_ref, sh_ref)
    for o_ref, (a, b) in zip(outs, segs):
        o_ref[...] = _dot(h, w_ref[:, a:b])


def proj_in_a(x, shift, scale, norm_w, w_bf16, segs, tm):
    bsz, t, d = x.shape
    n = bsz * t
    sh, tpg = _mod_rows(shift, bsz, t, tm)
    sc, _ = _mod_rows(scale, bsz, t, tm)
    r = sh.shape[1]
    outs = pl.pallas_call(
        functools.partial(_proj_a_body, segs=segs),
        out_shape=[jax.ShapeDtypeStruct((n, b - a), F32) for a, b in segs],
        grid=(n // tm,),
        in_specs=[pl.BlockSpec((tm, d), lambda i: (i, 0)),
                  pl.BlockSpec((1, r, d), lambda i: (i // tpg, 0, 0)),
                  pl.BlockSpec((1, r, d), lambda i: (i // tpg, 0, 0)),
                  pl.BlockSpec((1, d), lambda i: (0, 0)),
                  pl.BlockSpec(w_bf16.shape, lambda i: (0, 0))],
        out_specs=[pl.BlockSpec((tm, b - a), lambda i: (i, 0)) for a, b in segs],
        compiler_params=_cp(("arbitrary",)),
        name="proj_in_a",
    )(x.reshape(n, d), sh, sc, norm_w.reshape(1, d), w_bf16)
    return [o.reshape(bsz, t, -1) for o in outs]


def _proj_c_body(x_ref, sh_ref, sc_ref, nw_ref, w_ref, cs_ref, s1_ref, s2_ref, bg_ref,
                 q_ref, kc_ref, vc_ref, ks_ref, vs_ref, kw_ref, vw_ref, z_ref, gt_ref, *, segs):
    h = _norm_mod(x_ref, nw_ref, sc_ref, sh_ref)
    cs, s1, s2 = cs_ref[...], s1_ref[...], s2_ref[...]
    outs = (q_ref, kc_ref, vc_ref, ks_ref, vs_ref, kw_ref, vw_ref, z_ref, gt_ref)
    roped = (True, False, False, True, False, True, False, False, False)
    for o_ref, (a, b), rp in zip(outs, segs, roped):
        if rp:
            for j in range((b - a) // LANES):
                u = _dot(h, w_ref[:, a + j * LANES:a + (j + 1) * LANES])
                o_ref[:, j * LANES:(j + 1) * LANES] = _rope_tile(u, cs, s1, s2)
        elif o_ref is gt_ref:
            o_ref[...] = _sigmoid(_dot(h, w_ref[:, a:b]) + bg_ref[...])
        else:
            o_ref[...] = _dot(h, w_ref[:, a:b])


def proj_in_c(x, shift, scale, norm_w, w_bf16, segs, tables, b_gate, tm):
    bsz, t, d = x.shape
    n = bsz * t
    sh, tpg = _mod_rows(shift, bsz, t, tm)
    sc, _ = _mod_rows(scale, bsz, t, tm)
    r = sh.shape[1]
    ttiles = tables[0].shape[0] // tm
    outs = pl.pallas_call(
        functools.partial(_proj_c_body, segs=segs),
        out_shape=[jax.ShapeDtypeStruct((n, b - a), F32) for a, b in segs],
        grid=(n // tm,),
        in_specs=[pl.BlockSpec((tm, d), lambda i: (i, 0)),
                  pl.BlockSpec((1, r, d), lambda i: (i // tpg, 0, 0)),
                  pl.BlockSpec((1, r, d), lambda i: (i // tpg, 0, 0)),
                  pl.BlockSpec((1, d), lambda i: (0, 0)),
                  pl.BlockSpec(w_bf16.shape, lambda i: (0, 0))]
                 + [pl.BlockSpec((tm, LANES), lambda i: (i % ttiles, 0))] * 3
                 + [pl.BlockSpec(b_gate.shape, lambda i: (0, 0))],
        out_specs=[pl.BlockSpec((tm, b - a), lambda i: (i, 0)) for a, b in segs],
        compiler_params=_cp(("arbitrary",)),
        name="proj_in_c",
    )(x.reshape(n, d), sh, sc, norm_w.reshape(1, d), w_bf16, *tables, b_gate)
    return [o.reshape(bsz, t, -1) for o in outs]


def rope_tables(pos):
    half = ROPE_DIM // 2
    freq = jnp.power(ROPE_THETA, -2.0 * jnp.arange(half, dtype=F32) / ROPE_DIM)
    ang = pos.astype(F32)[:, None] * freq[None, :]
    cos, sin = jnp.cos(ang), jnp.sin(ang)
    n = pos.shape[0]
    one = jnp.ones((n, HALF - ROPE_DIM), F32)
    zero = jnp.zeros((n, HALF - ROPE_DIM), F32)
    zh = jnp.zeros((n, half), F32)
    cs = jnp.concatenate([cos, cos, one], axis=1)
    s1 = jnp.concatenate([-sin, zh, zero], axis=1)
    s2 = jnp.concatenate([zh, sin, zero], axis=1)
    return tuple(jnp.concatenate([a, a], axis=1) for a in (cs, s1, s2))


def _pool_body(xp_ref, zp_ref, buf_ref, w_ref, sc_ref, y_ref, nb_ref, hist, *, tt, pos0, nbuf):
    t = pl.program_id(1)
    base = nbuf + 1

    @pl.when(t == 0)
    def _():
        hist[0:1, :] = jnp.zeros((1, hist.shape[1]), F32)
        hist[1:base, :] = buf_ref[0]

    hist[base:base + tt, :] = xp_ref[0]
    pos = pos0 + t * tt + lax.broadcasted_iota(I32, (tt, 1), 0)
    gw = hist.shape[1] // len(POOL_WINDOWS)
    for g, w in enumerate(POOL_WINDOWS):
        ln = slice(g * gw, (g + 1) * gw)
        x = hist[base:base + tt, ln]
        acc = x
        for k in range(1, w):
            acc = acc + hist[base - k:base - k + tt, ln]
        cnt = jnp.minimum(pos + 1, w).astype(F32)
        yg = _dot(acc / cnt - x, w_ref[g]) * sc_ref[:, ln]
        y_ref[0, :, ln] = yg * _silu(zp_ref[0, :, ln])
    new = hist[tt + 1:tt + base, :]
    nb_ref[0] = new
    hist[1:base, :] = new


def pool_mix(xp, zp, buf, w_grp, scale, pos0, tt):
    bsz, t, pw = xp.shape
    nbuf = buf.shape[1]
    assert nbuf == max(POOL_WINDOWS) - 1
    return pl.pallas_call(
        functools.partial(_pool_body, tt=tt, pos0=pos0, nbuf=nbuf),
        out_shape=[jax.ShapeDtypeStruct((bsz, t, pw), F32), jax.ShapeDtypeStruct((bsz, nbuf, pw), F32)],
        grid=(bsz, t // tt),
        in_specs=[pl.BlockSpec((1, tt, pw), lambda b, i: (b, i, 0)),
                  pl.BlockSpec((1, tt, pw), lambda b, i: (b, i, 0)),
                  pl.BlockSpec((1, nbuf, pw), lambda b, i: (b, 0, 0)),
                  pl.BlockSpec(w_grp.shape, lambda b, i: (0, 0, 0)),
                  pl.BlockSpec((1, pw), lambda b, i: (0, 0))],
        out_specs=[pl.BlockSpec((1, tt, pw), lambda b, i: (b, i, 0)),
                   pl.BlockSpec((1, nbuf, pw), lambda b, i: (b, 0, 0))],
        scratch_shapes=[pltpu.VMEM((nbuf + 1 + tt, pw), F32)],
        compiler_params=_cp(("arbitrary", "arbitrary")),
        name="pool_mix",
    )(xp, zp, buf, w_grp, scale.reshape(1, pw))


def _mlstm_body(qk_ref, v_ref, o_ref, z_ref, g_ref, cb_ref, c0_ref, n0_ref, m0_ref, cw_ref, cbias_ref, bif_ref, hn_ref,
                y_ref, nc_ref, c_ref, n_ref, m_ref, chist, q_s, k_s, *, tt, chunk, heads, hd, ncb):
    t = pl.program_id(1)
    base = 8
    mw = heads * hd

    @pl.when(t == 0)
    def _():
        chist[0:base - ncb, :] = jnp.zeros((base - ncb, chist.shape[1]), F32)
        chist[base - ncb:base, :] = cb_ref[0]
        c_ref[...] = c0_ref[...]
        n_ref[...] = n0_ref[...]
        m_ref[...] = m0_ref[...]

    chist[base:base + tt, :] = qk_ref[0]
    rc = min(tt, 64)
    for r0 in range(0, tt, rc):
        for half, dst, mul in ((0, q_s, 1.0), (1, k_s, hd ** -0.5)):
            ln = slice(half * mw, (half + 1) * mw)
            acc = None
            for j in range(ncb + 1):
                term = chist[base - ncb + j + r0:base - ncb + j + r0 + rc, ln] * cw_ref[j:j + 1, ln]
                acc = term if acc is None else acc + term
            a = _silu(cbias_ref[:, ln] + acc)
            dst[r0:r0 + rc, :] = a * mul if half else a
    new = chist[tt + base - ncb:tt + base, :]
    nc_ref[0] = new
    chist[base - ncb:base, :] = new

    ii = lax.broadcasted_iota(I32, (chunk, chunk), 0)
    jj = lax.broadcasted_iota(I32, (chunk, chunk), 1)
    tril, eye, triu = jj <= ii, ii == jj, ii <= jj

    def chunk_step(c, carry):
        r0 = pl.multiple_of(c * chunk, chunk)
        rows = pl.ds(r0, chunk)
        gl = g_ref[0, rows, :] + bif_ref[...]
        lfa = -(jnp.maximum(-gl, 0.0) + jnp.log1p(jnp.exp(-jnp.abs(gl))))
        for h in range(heads):
            ln = slice(h * hd, (h + 1) * hd)
            ig_col = gl[:, h:h + 1]
            lf_col = lfa[:, heads + h:heads + h + 1]
            lf_row = jnp.sum(jnp.where(eye, lf_col, 0.0), axis=0, keepdims=True)
            ig_row = jnp.sum(jnp.where(eye, ig_col, 0.0), axis=0, keepdims=True)
            b_col = jnp.sum(jnp.where(tril, lf_row, 0.0), axis=1, keepdims=True)
            b_row = jnp.sum(jnp.where(triu, lf_col, 0.0), axis=0, keepdims=True)
            m_prev = m_ref[0, :, h:h + 1]
            dm = jnp.where(tril, b_col - b_row + ig_row, -jnp.inf)
            a_int = b_col + m_prev
            mt = jnp.maximum(a_int, jnp.max(dm, axis=1, keepdims=True))
            w_int = jnp.exp(a_int - mt)
            qh, kh, vh = q_s[rows, ln], k_s[rows, ln], v_ref[0, rows, ln]
            s = _nt(qh, kh) * jnp.exp(dm - mt)
            ch = c_ref[0, h]
            nh = n_ref[0, h:h + 1, :]
            num = w_int * _dot(qh, ch) + _dot(s, vh)
            den = w_int * jnp.sum(qh * nh, axis=1, keepdims=True) + jnp.sum(s, axis=1, keepdims=True)
            hh = num / jnp.maximum(jnp.abs(den), jnp.exp(-mt))
            bl = b_col[chunk - 1:chunk, :]
            g_col = bl - b_col + ig_col
            m_new = jnp.maximum(bl + m_prev, jnp.max(g_col, axis=0, keepdims=True))
            decay = jnp.exp(bl + m_prev - m_new)
            wk = jnp.exp(g_col - m_new) * kh
            c_ref[0, h] = decay * ch + _tn(wk, vh)
            n_ref[0, h:h + 1, :] = decay * nh + jnp.sum(wk, axis=0, keepdims=True)
            m_ref[0, :, h:h + 1] = m_new
            hn = hh * lax.rsqrt(jnp.mean(hh * hh, axis=-1, keepdims=True) + EPS) * hn_ref[:, ln]
            y_ref[0, rows, ln] = hn * _sigmoid(o_ref[0, rows, ln]) * _silu(z_ref[0, rows, ln])
        return carry

    lax.fori_loop(0, tt // chunk, chunk_step, 0)


def mlstm_mix(qk, v, o, z, gif, conv_buf, c0, n0, m0, conv_w, conv_b, b_if, head_norm, tt):
    bsz, t, mw2 = qk.shape
    mw = mw2 // 2
    heads, hd = c0.shape[1], c0.shape[2]
    ncb = conv_buf.shape[1]
    chunk = min(MLSTM_CHUNK, t)
    gl = gif.shape[-1]
    bif = jnp.zeros((1, gl), F32).at[0, :2 * heads].set(b_if)
    tile = lambda w: pl.BlockSpec((1, tt, w), lambda b, i: (b, i, 0))
    per_b = lambda shp: pl.BlockSpec((1,) + shp, lambda b, i: (b,) + (0,) * len(shp))
    const = lambda shp: pl.BlockSpec(shp, lambda b, i: (0,) * len(shp))
    y, nconv, c, n, m = pl.pallas_call(
        functools.partial(_mlstm_body, tt=tt, chunk=chunk, heads=heads, hd=hd, ncb=ncb),
        out_shape=[jax.ShapeDtypeStruct((bsz, t, mw), F32), jax.ShapeDtypeStruct((bsz, ncb, mw2), F32),
                   jax.ShapeDtypeStruct(c0.shape, F32), jax.ShapeDtypeStruct(n0.shape, F32),
                   jax.ShapeDtypeStruct((bsz, 1, heads), F32)],
        grid=(bsz, t // tt),
        in_specs=[tile(mw2), tile(mw), tile(mw), tile(mw), tile(gl), per_b((ncb, mw2)),
                  per_b((heads, hd, hd)), per_b((heads, hd)), per_b((1, heads)),
                  const(conv_w.shape), const((1, mw2)), const((1, gl)), const((1, mw))],
        out_specs=[tile(mw), per_b((ncb, mw2)), per_b((heads, hd, hd)), per_b((heads, hd)), per_b((1, heads))],
        scratch_shapes=[pltpu.VMEM((8 + tt, mw2), F32), pltpu.VMEM((tt, mw), F32), pltpu.VMEM((tt, mw), F32)],
        compiler_params=_cp(("arbitrary", "arbitrary")),
        name="mlstm_mix",
    )(qk, v, o, z, gif, conv_buf, c0, n0, m0.reshape(bsz, 1, heads), conv_w, conv_b.reshape(1, mw2), bif,
      head_norm.reshape(1, mw))
    return y, nconv, c, n, m.reshape(bsz, heads)


def _proj_out_a_body(x_ref, yp_ref, ym_ref, gate_ref, wa_ref, wb_ref, o_ref):
    y = _dot(yp_ref[...].astype(BF16), wa_ref[...]) + _dot(ym_ref[...].astype(BF16), wb_ref[...])
    o_ref[...] = x_ref[...] + gate_ref[0] * y


def proj_out_a(x, yp, ym, gate, wa, wb, tm):
    bsz, t, d = x.shape
    n = bsz * t
    gt, tpg = _mod_rows(gate, bsz, t, tm)
    r = gt.shape[1]
    pw, mw = yp.shape[-1], ym.shape[-1]
    out = pl.pallas_call(
        _proj_out_a_body,
        out_shape=jax.ShapeDtypeStruct((n, d), F32),
        grid=(n // tm,),
        in_specs=[pl.BlockSpec((tm, d), lambda i: (i, 0)), pl.BlockSpec((tm, pw), lambda i: (i, 0)),
                  pl.BlockSpec((tm, mw), lambda i: (i, 0)), pl.BlockSpec((1, r, d), lambda i: (i // tpg, 0, 0)),
                  pl.BlockSpec(wa.shape, lambda i: (0, 0)), pl.BlockSpec(wb.shape, lambda i: (0, 0))],
        out_specs=pl.BlockSpec((tm, d), lambda i: (i, 0)),
        compiler_params=_cp(("arbitrary",)),
        name="proj_out_a",
    )(x.reshape(n, d), yp.reshape(n, pw), ym.reshape(n, mw), gt, wa, wb)
    return out.reshape(bsz, t, d)


def _proj_out_c_body(x_ref, o_ref, z_ref, gate_ref, w_ref, fn_ref, y_ref, *, final):
    a = (o_ref[...] * _silu(z_ref[...])).astype(BF16)
    y = x_ref[...] + gate_ref[0] * _dot(a, w_ref[...])
    y_ref[...] = _rms(y, fn_ref[...]) if final else y


def proj_out_c(x, o, z, gate, w, final_norm, final, tm):
    bsz, t, d = x.shape
    n = bsz * t
    gt, tpg = _mod_rows(gate, bsz, t, tm)
    r = gt.shape[1]
    nw = o.shape[-1]
    y = pl.pallas_call(
        functools.partial(_proj_out_c_body, final=final),
        out_shape=jax.ShapeDtypeStruct((n, d), F32),
        grid=(n // tm,),
        in_specs=[pl.BlockSpec((tm, d), lambda i: (i, 0)), pl.BlockSpec((tm, nw), lambda i: (i, 0)),
                  pl.BlockSpec((tm, nw), lambda i: (i, 0)), pl.BlockSpec((1, r, d), lambda i: (i // tpg, 0, 0)),
                  pl.BlockSpec(w.shape, lambda i: (0, 0)), pl.BlockSpec((1, d), lambda i: (0, 0))],
        out_specs=pl.BlockSpec((tm, d), lambda i: (i, 0)),
        compiler_params=_cp(("arbitrary",)),
        name="proj_out_c",
    )(x.reshape(n, d), o.reshape(n, nw), z.reshape(n, nw), gt, w, final_norm.reshape(1, d))
    return y.reshape(bsz, t, d)


def _cmp_pool_body(*refs, rows, paged):
    if paged:
        refs = refs[1:]
    k_ref, v_ref, kp_ref, vp_ref, klo, khi, vlo, vhi = refs
    rc = min(rows, 128)
    for src, pos, lo, hi in ((k_ref, kp_ref, klo, khi), (v_ref, vp_ref, vlo, vhi)):
        wl = pos[0:CMP_STRIDE, :].reshape(1, CMP_STRIDE, 1)
        wh = pos[CMP_STRIDE:CMP_BLOCK, :].reshape(1, CMP_STRIDE, 1)
        for r0 in range(0, rows, rc):
            x = src[0, r0:r0 + rc, :].reshape(rc // CMP_STRIDE, CMP_STRIDE, src.shape[-1])
            o = slice(r0 // CMP_STRIDE, (r0 + rc) // CMP_STRIDE)
            lo[0, o, :] = jnp.sum(x * wl, axis=1)
            hi[0, o, :] = jnp.sum(x * wh, axis=1)


def cmp_pool_dense(k_rows, v_rows, kpos, vpos):
    bsz, s, w = k_rows.shape
    n_sub = s // CMP_STRIDE
    rows = n_sub * CMP_STRIDE
    out = jax.ShapeDtypeStruct((bsz, n_sub, w), F32)
    return pl.pallas_call(
        functools.partial(_cmp_pool_body, rows=rows, paged=False),
        out_shape=[out] * 4,
        grid=(bsz,),
        in_specs=[pl.BlockSpec((1, rows, w), lambda b: (b, 0, 0))] * 2 + [pl.BlockSpec((CMP_BLOCK, 1), lambda b: (0, 0))] * 2,
        out_specs=[pl.BlockSpec((1, n_sub, w), lambda b: (b, 0, 0))] * 4,
        compiler_params=_cp(("arbitrary",)),
        name="cmp_pool_dense",
    )(k_rows, v_rows, kpos.reshape(CMP_BLOCK, 1), vpos.reshape(CMP_BLOCK, 1))


def cmp_pool_paged(page_table, pool_k, pool_v, kpos, vpos):
    bsz, n_pages = page_table.shape
    _, page, w = pool_k.shape
    spp = page // CMP_STRIDE
    out = jax.ShapeDtypeStruct((bsz, n_pages * spp, w), F32)
    return pl.pallas_call(
        functools.partial(_cmp_pool_body, rows=page, paged=True),
        out_shape=[out] * 4,
        grid_spec=pltpu.PrefetchScalarGridSpec(
            num_scalar_prefetch=1,
            grid=(bsz, n_pages),
            in_specs=[pl.BlockSpec((1, page, w), lambda b, p, pt: (pt[b, p], 0, 0))] * 2
                     + [pl.BlockSpec((CMP_BLOCK, 1), lambda b, p, pt: (0, 0))] * 2,
            out_specs=[pl.BlockSpec((1, spp, w), lambda b, p, pt: (b, p, 0))] * 4),
        compiler_params=_cp(("arbitrary", "arbitrary")),
        name="cmp_pool_paged",
    )(page_table, pool_k, pool_v, kpos.reshape(CMP_BLOCK, 1), vpos.reshape(CMP_BLOCK, 1))


def _gelu_tanh(x):
    return x * (0.5 * (1.0 + jnp.tanh(np.sqrt(2.0 / np.pi).astype(np.float32) * (x + 0.044715 * (x * x * x)))))


def _cmp_mlp_body(klo, khi, vlo, vhi, kw1, kb1, kw2, vw1, vb1, vw2, cs_ref, s1_ref, s2_ref, kc_ref, vc_ref, sh, *, n):
    rc = min(n, 256)
    for lo, hi, w1, b1, w2, out, rope in ((klo, khi, kw1, kb1, kw2, kc_ref, True), (vlo, vhi, vw1, vb1, vw2, vc_ref, False)):
        sh[0:n, :] = hi[0]
        sh[n:n + 8, :] = jnp.zeros((8, sh.shape[1]), F32)
        for r0 in range(0, n, rc):
            pooled = lo[0, r0:r0 + rc, :] + sh[r0 + 1:r0 + 1 + rc, :]
            hid = _gelu_tanh(_dot(pooled, w1[...]) + b1[...])
            y = _dot(hid, w2[...])
            if rope:
                for j in range(y.shape[1] // LANES):
                    ln = slice(j * LANES, (j + 1) * LANES)
                    out[0, r0:r0 + rc, ln] = _rope_tile(y[:, ln], cs_ref[r0:r0 + rc, :], s1_ref[r0:r0 + rc, :], s2_ref[r0:r0 + rc, :])
            else:
                out[0, r0:r0 + rc, :] = y


def cmp_mlp(klo, khi, vlo, vhi, kw1, kb1, kw2, vw1, vb1, vw2):
    bsz, n, w = klo.shape
    groups = w // kw1.shape[0]
    eye = jnp.eye(groups, dtype=F32)
    bd = lambda m: jnp.kron(eye, m)
    tl = lambda v: jnp.tile(v, groups).reshape(1, w)
    tables = rope_tables(jnp.arange(n) * CMP_STRIDE + CMP_BLOCK - 1)
    per_b = pl.BlockSpec((1, n, w), lambda b: (b, 0, 0))
    mat = pl.BlockSpec((w, w), lambda b: (0, 0))
    vec = pl.BlockSpec((1, w), lambda b: (0, 0))
    tab = pl.BlockSpec((n, LANES), lambda b: (0, 0))
    return pl.pallas_call(
        functools.partial(_cmp_mlp_body, n=n),
        out_shape=[jax.ShapeDtypeStruct((bsz, n, w), F32)] * 2,
        grid=(bsz,),
        in_specs=[per_b] * 4 + [mat, vec, mat, mat, vec, mat] + [tab] * 3,
        out_specs=[per_b] * 2,
        scratch_shapes=[pltpu.VMEM((n + 8, w), F32)],
        compiler_params=_cp(("arbitrary",)),
        name="cmp_mlp",
    )(klo, khi, vlo, vhi, bd(kw1), tl(kb1), bd(kw2), bd(vw1), tl(vb1), bd(vw2), *tables)


def overlap_matrix(n_cmp_pad, n_cmp, n_sel, n_sel_pad):
    cs = np.arange(n_cmp_pad)[:, None] * CMP_STRIDE
    ss = np.arange(n_sel_pad)[None, :] * SEL_BLOCK
    ov = (cs < ss + SEL_BLOCK) & (cs + CMP_BLOCK > ss)
    ov &= (np.arange(n_cmp_pad)[:, None] < n_cmp) & (np.arange(n_sel_pad)[None, :] < n_sel)
    return jnp.asarray(ov, BF16)


def _sel_scores(imp, tcol):
    srow = lax.broadcasted_iota(I32, (1, imp.shape[1]), 1)
    cur = lax.shift_right_logical(tcol, int(np.log2(SEL_BLOCK)))
    valid = srow <= cur
    forced = (srow == 0) | (srow == cur) | (srow == cur - 1)
    return jnp.where(valid, imp + jnp.where(forced, FORCE, 0.0), NEG)


def _attn_prompt_body(q_ref, gt_ref, kc_ref, vc_ref, ks_ref, vs_ref, kw_ref, vw_ref, ov_ref, e_ref, o_ref,
                      ksb, vsb, kwb, vwb, kcb, vcb, msk, *, n_sel):
    g = pl.program_id(1)
    qi = pl.program_id(2)
    odd = (g % 2) == 1
    t_all = ksb.shape[0]

    @pl.when(qi == 0)
    def _():
        rc = 512
        for src, dst in ((ks_ref, ksb), (vs_ref, vsb), (kw_ref, kwb), (vw_ref, vwb)):
            for r0 in range(0, t_all, rc):
                dst[r0:r0 + rc, :] = src[0, r0:r0 + rc, :].astype(BF16)
        kcb[...] = kc_ref[0].astype(BF16)
        vcb[...] = vc_ref[0].astype(BF16)

    tcol = qi * TQ + lax.broadcasted_iota(I32, (TQ, 1), 0)
    lane = lax.broadcasted_iota(I32, (1, LANES), 1)
    gt = gt_ref[0]
    heads = q_ref.shape[-1] // HALF
    qm = [_to_half(q_ref[0, :, (r // 2) * LANES:(r // 2 + 1) * LANES] * (HALF ** -0.5), (r % 2) == 1, odd).astype(BF16)
          for r in range(heads)]

    mask_c = (lane * CMP_STRIDE + (CMP_BLOCK - 1)) <= tcol
    psum = jnp.zeros((TQ, LANES), F32)
    out = []
    for r in range(heads):
        p, l = _masked_softmax_parts(_nt(qm[r], kcb[...]), mask_c)
        p = p / jnp.maximum(l, 1e-30)
        psum = psum + p
        out.append(gt[:, r:r + 1] * _dot(p.astype(BF16), vcb[...]))

    score = _sel_scores(_split_dot(psum, ov_ref[...]), tcol)
    rank = jnp.zeros((TQ, LANES), I32)
    for s2 in range(n_sel):
        col = score[:, s2:s2 + 1]
        rank = rank + ((col > score) | ((col == score) & (s2 < lane))).astype(I32)
    selm = jnp.where((rank < SEL_TOP) & (score > 0.5 * NEG), 1.0, 0.0).astype(BF16)
    for kt in range(e_ref.shape[0]):
        msk[kt] = _dot(selm, e_ref[kt])

    def flash(r, kb, vb, lo, hi, allowed):
        def step(kt, carry):
            m, l, acc = carry
            rows = pl.ds(pl.multiple_of(kt * TQ, TQ), TQ)
            ok = allowed(kt, kt * TQ + lane)
            s = jnp.where(ok, _nt(qm[r], kb[rows, :]), NEG)
            m_new = jnp.maximum(m, jnp.max(s, axis=-1, keepdims=True))
            alpha = jnp.exp(m - m_new)
            p = jnp.where(ok, jnp.exp(s - m_new), 0.0)
            return (m_new, alpha * l + jnp.sum(p, axis=-1, keepdims=True),
                    alpha * acc + _dot(p.astype(BF16), vb[rows, :]))

        m, l, acc = lax.fori_loop(lo, hi, step, (jnp.full((TQ, 1), NEG, F32), jnp.zeros((TQ, 1), F32),
                                                 jnp.zeros((TQ, LANES), F32)))
        return acc / jnp.maximum(l, 1e-30)

    nb = heads
    w_lo = jnp.maximum(qi - WINDOW // TQ, 0)
    for r in range(heads):
        o_s = flash(r, ksb, vsb, 0, qi + 1, lambda kt, kpos: (msk[kt] > 0.5) & (kpos <= tcol))
        o_w = flash(r, kwb, vwb, w_lo, qi + 1, lambda kt, kpos: (kpos <= tcol) & (kpos > tcol - WINDOW))
        out[r] = out[r] + gt[:, nb + r:nb + r + 1] * o_s + gt[:, 2 * nb + r:2 * nb + r + 1] * o_w
    for pi in range(heads // 2):
        o_ref[0, :, pi * LANES:(pi + 1) * LANES] = (_to_half(out[2 * pi], odd, False) + _to_half(out[2 * pi + 1], odd, True))


def attn_prompt(q, gt, kc, vc, ks, vs, kw, vw, n_cmp):
    bsz, t, width = q.shape
    kvw = ks.shape[-1]
    groups = kvw // HALF
    gq = width // groups
    n_cmp_pad = kc.shape[1]
    n_sel = t // SEL_BLOCK
    assert t % TQ == 0 and TQ == LANES and WINDOW % TQ == 0 and n_cmp_pad == LANES and n_sel <= LANES
    nkt = t // TQ
    ov = overlap_matrix(n_cmp_pad, n_cmp, n_sel, LANES)
    e3 = jnp.asarray((np.arange(t)[None, :] // SEL_BLOCK == np.arange(LANES)[:, None]).reshape(LANES, nkt, TQ).transpose(1, 0, 2), BF16)
    pair = lambda rows: pl.BlockSpec((1, rows, LANES), lambda b, g, i: (b, 0, g // 2))
    return pl.pallas_call(
        functools.partial(_attn_prompt_body, n_sel=n_sel),
        out_shape=jax.ShapeDtypeStruct((bsz, t, width), F32),
        grid=(bsz, groups, nkt),
        in_specs=[pl.BlockSpec((1, TQ, gq), lambda b, g, i: (b, i, g)),
                  pl.BlockSpec((1, TQ, LANES), lambda b, g, i: (b, i, g)),
                  pair(n_cmp_pad), pair(n_cmp_pad), pair(t), pair(t), pair(t), pair(t),
                  pl.BlockSpec(ov.shape, lambda b, g, i: (0, 0)),
                  pl.BlockSpec(e3.shape, lambda b, g, i: (0, 0, 0))],
        out_specs=pl.BlockSpec((1, TQ, gq), lambda b, g, i: (b, i, g)),
        scratch_shapes=[pltpu.VMEM((t, LANES), BF16)] * 4 + [pltpu.VMEM((n_cmp_pad, LANES), BF16)] * 2
                       + [pltpu.VMEM((nkt, TQ, LANES), F32)],
        compiler_params=_cp(("arbitrary", "arbitrary", "arbitrary")),
        name="attn_prompt",
    )(q, gt, kc, vc, ks, vs, kw, vw, ov, e3)


def _attn_select_body(q_ref, gt_ref, kc_ref, vc_ref, ov_ref, oc_ref, idx_ref, *, past, groups):
    t = q_ref.shape[1]
    n_cmp_pad = kc_ref.shape[1]
    n_sel_pad = ov_ref.shape[1]
    gq = q_ref.shape[-1] // groups
    heads = gq // HALF
    tcol = past + lax.broadcasted_iota(I32, (t, 1), 0)
    jrow = lax.broadcasted_iota(I32, (1, n_cmp_pad), 1)
    mask_c = (jrow * CMP_STRIDE + (CMP_BLOCK - 1)) <= tcol
    si = lax.broadcasted_iota(I32, (n_sel_pad, n_sel_pad), 0)
    sj = lax.broadcasted_iota(I32, (n_sel_pad, n_sel_pad), 1)
    kk = lax.broadcasted_iota(I32, (SEL_TOP, n_sel_pad), 0)
    ss = lax.broadcasted_iota(I32, (SEL_TOP, n_sel_pad), 1)
    lane = lax.broadcasted_iota(I32, (SEL_TOP, LANES), 1)
    idx = jnp.zeros((SEL_TOP, LANES), I32)
    for g in range(groups):
        pt = slice((g // 2) * LANES, (g // 2 + 1) * LANES)
        kcb = kc_ref[0, :, pt].astype(BF16)
        vcb = vc_ref[0, :, pt].astype(BF16)
        odd = bool(g % 2)
        psum = jnp.zeros((t, n_cmp_pad), F32)
        outs = []
        for r in range(heads):
            qt = q_ref[0, :, g * gq + (r // 2) * LANES:g * gq + (r // 2 + 1) * LANES] * (HALF ** -0.5)
            qm = _to_half(qt, bool(r % 2), odd).astype(BF16)
            p, l = _masked_softmax_parts(_nt(qm, kcb), mask_c)
            p = p / jnp.maximum(l, 1e-30)
            psum = psum + p
            outs.append(gt_ref[0, :, g * LANES + r:g * LANES + r + 1] * _dot(p.astype(BF16), vcb))
        for pi in range(heads // 2):
            oc_ref[0, :, g * gq + pi * LANES:g * gq + (pi + 1) * LANES] = (
                _to_half(outs[2 * pi], odd, False) + _to_half(outs[2 * pi + 1], odd, True))
        score = _sel_scores(_split_dot(psum, ov_ref[...]), tcol)
        score_t = score.T
        for tk in range(t):
            col, row = score_t[:, tk:tk + 1], score[tk:tk + 1, :]
            ahead = (col > row) | ((col == row) & (si < sj))
            rank = jnp.sum(ahead.astype(I32), axis=0, keepdims=True)
            top = jnp.sum(jnp.where(rank == kk, ss, 0), axis=1, keepdims=True)
            idx = jnp.where(lane == tk * groups + g, top, idx)
    idx_ref[0] = idx


def attn_select(q, gt, kc, vc, past, n_cmp, n_sel):
    bsz, t, width = q.shape
    groups = kc.shape[-1] // HALF
    n_cmp_pad = kc.shape[1]
    n_sel_pad = -(-n_sel // LANES) * LANES
    assert t * groups <= LANES
    ov = overlap_matrix(n_cmp_pad, n_cmp, n_sel, n_sel_pad)
    per_b = lambda a: pl.BlockSpec((1,) + a.shape[1:], lambda b: (b, 0, 0))
    return pl.pallas_call(
        functools.partial(_attn_select_body, past=past, groups=groups),
        out_shape=[jax.ShapeDtypeStruct((bsz, t, width), F32), jax.ShapeDtypeStruct((bsz, SEL_TOP, LANES), I32)],
        grid=(bsz,),
        in_specs=[per_b(q), per_b(gt), per_b(kc), per_b(vc), pl.BlockSpec(ov.shape, lambda b: (0, 0))],
        out_specs=[pl.BlockSpec((1, t, width), lambda b: (b, 0, 0)), pl.BlockSpec((1, SEL_TOP, LANES), lambda b: (b, 0, 0))],
        compiler_params=_cp(("arbitrary",)),
        name="attn_select",
    )(q, gt, kc, vc, ov)


def _attn_gather_body(idx_ref, pt_ref, q_ref, gt_ref, oc_ref, wk_ref, wv_ref, nk_ref, nv_ref,
                      pk_ref, pv_ref, sk_ref, sv_ref, o_ref, kbuf, vbuf, sem, *, past, groups, bpp, n_past_blk):
    b = pl.program_id(0)
    g = pl.program_id(1)
    t = q_ref.shape[1]
    n_pages = pt_ref.shape[1]
    n_new_blk = sk_ref.shape[1] // SEL_BLOCK
    heads = q_ref.shape[-1] // HALF
    odd = (g % 2) == 1
    lanes = pl.ds(pl.multiple_of((g // 2) * LANES, LANES), LANES)
    n_slots = t * SEL_TOP

    def copies(slot):
        tk, k = slot // SEL_TOP, slot % SEL_TOP
        blk = idx_ref[b, k, tk * groups + g]
        page = pt_ref[b, jnp.minimum(blk // bpp, n_pages - 1)]
        row = page * bpp + blk % bpp
        nb = jnp.clip(blk - n_past_blk, 0, n_new_blk - 1)
        new_rows = pl.ds(pl.multiple_of(nb * SEL_BLOCK, SEL_BLOCK), SEL_BLOCK)
        past_cp = (pltpu.make_async_copy(pk_ref.at[row, :, lanes], kbuf.at[slot], sem.at[0]),
                   pltpu.make_async_copy(pv_ref.at[row, :, lanes], vbuf.at[slot], sem.at[1]))
        new_cp = (pltpu.make_async_copy(sk_ref.at[b, new_rows, lanes], kbuf.at[slot], sem.at[0]),
                  pltpu.make_async_copy(sv_ref.at[b, new_rows, lanes], vbuf.at[slot], sem.at[1]))
        return blk < n_past_blk, past_cp, new_cp

    def start(slot, carry):
        in_past, past_cp, new_cp = copies(slot)

        @pl.when(in_past)
        def _():
            for c in past_cp:
                c.start()

        @pl.when(jnp.logical_not(in_past))
        def _():
            for c in new_cp:
                c.start()

        return carry

    lax.fori_loop(0, n_slots, start, 0)

    qall = jnp.concatenate(
        [_to_half(q_ref[0, :, (r // 2) * LANES:(r // 2 + 1) * LANES] * (HALF ** -0.5), (r % 2) == 1, odd) for r in range(heads)],
        axis=0).astype(BF16)
    rows = heads * t
    tok = lax.broadcasted_iota(I32, (rows, 1), 0) % t
    qpos = past + tok

    wb = wk_ref.shape[1]
    ci = lax.broadcasted_iota(I32, (1, wb), 1) + (past - wb)
    ni = lax.broadcasted_iota(I32, (1, t), 1) + past
    mask1 = (ci <= qpos) & (ci > qpos - WINDOW) & (ci >= 0)
    mask2 = (ni <= qpos) & (ni > qpos - WINDOW)
    s1 = jnp.where(mask1, _nt(qall, wk_ref[0].astype(BF16)), NEG)
    s2 = jnp.where(mask2, _nt(qall, nk_ref[0].astype(BF16)), NEG)
    m = jnp.maximum(jnp.max(s1, axis=-1, keepdims=True), jnp.max(s2, axis=-1, keepdims=True))
    p1 = jnp.where(mask1, jnp.exp(s1 - m), 0.0)
    p2 = jnp.where(mask2, jnp.exp(s2 - m), 0.0)
    l = jnp.sum(p1, axis=-1, keepdims=True) + jnp.sum(p2, axis=-1, keepdims=True)
    o_w = (_dot(p1.astype(BF16), wv_ref[0].astype(BF16)) + _dot(p2.astype(BF16), nv_ref[0].astype(BF16))) / jnp.maximum(l, 1e-30)

    def wait(slot, carry):
        in_past, past_cp, new_cp = copies(slot)

        @pl.when(in_past)
        def _():
            for c in past_cp:
                c.wait()

        @pl.when(jnp.logical_not(in_past))
        def _():
            for c in new_cp:
                c.wait()

        return carry

    lax.fori_loop(0, n_slots, wait, 0)

    seg = lax.broadcasted_iota(I32, (1, SEL_TOP * SEL_BLOCK), 1) // SEL_BLOCK
    off = lax.broadcasted_iota(I32, (1, SEL_TOP * SEL_BLOCK), 1) % SEL_BLOCK
    o_s = jnp.zeros((rows, LANES), F32)
    for tk in range(t):
        kt = kbuf[tk * SEL_TOP:(tk + 1) * SEL_TOP].reshape(SEL_TOP * SEL_BLOCK, LANES).astype(BF16)
        vt = vbuf[tk * SEL_TOP:(tk + 1) * SEL_TOP].reshape(SEL_TOP * SEL_BLOCK, LANES).astype(BF16)
        blkv = jnp.zeros((1, SEL_TOP * SEL_BLOCK), I32)
        for k in range(SEL_TOP):
            blkv = jnp.where(seg == k, idx_ref[b, k, tk * groups + g], blkv)
        mask = (blkv * SEL_BLOCK + off) <= (past + tk)
        p, l = _masked_softmax_parts(_nt(qall, kt), mask)
        o = _dot(p.astype(BF16), vt) / jnp.maximum(l, 1e-30)
        o_s = jnp.where(tok == tk, o, o_s)

    nb = heads
    for pi in range(heads // 2):
        parts = []
        for r in (2 * pi, 2 * pi + 1):
            rr = slice(r * t, (r + 1) * t)
            tot = gt_ref[0, :, nb + r:nb + r + 1] * o_s[rr] + gt_ref[0, :, 2 * nb + r:2 * nb + r + 1] * o_w[rr]
            parts.append(_to_half(tot, odd, (r % 2) == 1))
        ln = slice(pi * LANES, (pi + 1) * LANES)
        o_ref[0, :, ln] = oc_ref[0, :, ln] + parts[0] + parts[1]


def attn_gather(idx, page_table, q, gt, oc, win_k, win_v, new_wk, new_wv, pool_k, pool_v, new_sk, new_sv, past):
    bsz, t, width = q.shape
    kvw = win_k.shape[-1]
    groups = kvw // HALF
    gq = width // groups
    page = pool_k.shape[1]
    bpp = page // SEL_BLOCK
    n_past_blk = past // SEL_BLOCK
    wb = win_k.shape[1]
    pk = pool_k.reshape(-1, SEL_BLOCK, kvw)
    pv = pool_v.reshape(-1, SEL_BLOCK, kvw)
    qblk = pl.BlockSpec((1, t, gq), lambda b, g, *_: (b, 0, g))
    pair = lambda rows: pl.BlockSpec((1, rows, LANES), lambda b, g, *_: (b, 0, g // 2))
    anyspec = pl.BlockSpec(memory_space=pl.ANY)
    return pl.pallas_call(
        functools.partial(_attn_gather_body, past=past, groups=groups, bpp=bpp, n_past_blk=n_past_blk),
        out_shape=jax.ShapeDtypeStruct((bsz, t, width), F32),
        grid_spec=pltpu.PrefetchScalarGridSpec(
            num_scalar_prefetch=2,
            grid=(bsz, groups),
            in_specs=[qblk, pl.BlockSpec((1, t, LANES), lambda b, g, *_: (b, 0, g)), qblk,
                      pair(wb), pair(wb), pair(t), pair(t), anyspec, anyspec, anyspec, anyspec],
            out_specs=qblk,
            scratch_shapes=[pltpu.VMEM((t * SEL_TOP, SEL_BLOCK, LANES), F32)] * 2 + [pltpu.SemaphoreType.DMA((2,))]),
        compiler_params=_cp(("arbitrary", "arbitrary")),
        name="attn_gather",
    )(idx, page_table, q, gt, oc, win_k, win_v, new_wk, new_wv, pk, pv, new_sk, new_sv)


def _layer_a_weights(w_in, heads, pw, mw):
    d, n = w_in.shape
    assert n == 2 * pw + 5 * mw + 2 * heads
    w = jnp.concatenate([w_in, jnp.zeros((d, LANES - 2 * heads), F32)], axis=1).astype(BF16)
    cuts = np.cumsum([0, pw, pw, 2 * mw, mw, mw, mw, LANES]).tolist()
    return w, list(zip(cuts[:-1], cuts[1:]))


def layer_a(x, mods, pos0, pool_buf, conv_buf, c0, n0, m0, norm_w, w_in, segs, b_if, pool_w, pool_scale,
            conv_w, conv_b, head_norm, wa, wb, tm, tt):
    shift, scale, gate = mods
    xp, zp, qk, v, o, z, gif = proj_in_a(x, shift, scale, norm_w, w_in, segs, tm)
    y_pool, new_pool = pool_mix(xp, zp, pool_buf, pool_w, pool_scale, pos0, tt)
    y_m, new_conv, c, n, m = mlstm_mix(qk, v, o, z, gif, conv_buf, c0, n0, m0, conv_w, conv_b, b_if, head_norm, tt)
    return proj_out_a(x, y_pool, y_m, gate, wa, wb, tm), new_pool, new_conv, c, n, m


def _layer_c_weights(w_in, b_gate, heads, kvw):
    d = w_in.shape[0]
    width = heads * HALF
    groups = kvw // HALF
    rep = heads // groups
    ng = 3 * heads
    cuts = np.cumsum([0, width] + [kvw] * 6 + [ng, width])
    q, kvs, gcols, z = w_in[:, :cuts[1]], w_in[:, cuts[1]:cuts[7]], w_in[:, cuts[7]:cuts[8]], w_in[:, cuts[8]:]
    src = np.arange(ng)
    gg, rr, br = src // (rep * 3), (src // 3) % rep, src % 3
    dst = gg * LANES + br * rep + rr
    gw = jnp.zeros((d, groups * LANES), F32).at[:, dst].set(gcols)
    bg = jnp.zeros((1, groups * LANES), F32).at[0, dst].set(b_gate)
    w = jnp.concatenate([q, kvs, z, gw], axis=1).astype(BF16)
    cuts2 = np.cumsum([0, width] + [kvw] * 6 + [width, groups * LANES]).tolist()
    return w, list(zip(cuts2[:-1], cuts2[1:])), bg


def kernel(x_prompt, x_sample, c_prompt, c_sample, state_pool, state_conv, state_C, state_n, state_m, cache_cmp_k, cache_cmp_v, cache_sel_k, cache_sel_v, cache_win_k, cache_win_v, page_table, a_norm, a_ada_w, a_ada_b, a_w_in, a_b_if, a_pool_w, a_pool_scale, a_conv_w, a_conv_b, a_head_norm, a_w_out, c_norm, c_ada_w, c_ada_b, c_w_in, c_b_gate, c_cmp_k_pos, c_cmp_k_w1, c_cmp_k_b1, c_cmp_k_w2, c_cmp_v_pos, c_cmp_v_w1, c_cmp_v_b1, c_cmp_v_w2, c_w_out, final_norm):
    bp, tp, d = x_prompt.shape
    bs, ts, _ = x_sample.shape
    na, nc = a_norm.shape[0], c_norm.shape[0]
    assert na == nc, "the final norm is fused into the last (attention) layer"
    n_pages, page = page_table.shape[1], cache_cmp_k.shape[2]
    past = n_pages * page
    kv_groups, hd = cache_cmp_k.shape[3], cache_cmp_k.shape[4]
    kvw = kv_groups * hd
    assert hd == HALF
    nsa_heads = c_b_gate.shape[1] // 3
    mheads = state_C.shape[2]
    pw = state_pool.shape[-1]
    tm_p, tt_p = 256, 256
    tm_s = bs * ts

    c_all = jnp.concatenate([c_prompt, c_sample], axis=0)
    xp, xs = x_prompt, x_sample
    outs_a = [[] for _ in range(10)]
    outs_c = [[] for _ in range(12)]
    for layer in range(na + nc):
        i = layer // 2
        if layer % 2 == 0:
            mods = ada_mod(c_all, a_ada_w[i], a_ada_b[i])
            mp, ms = [m[:bp] for m in mods], [m[bp:] for m in mods]
            w_in, segs = _layer_a_weights(a_w_in[i], mheads, pw, mheads * state_C.shape[3])
            wa, wb = a_w_out[i][:pw].astype(BF16), a_w_out[i][pw:].astype(BF16)
            shared = (a_norm[i], w_in, segs, a_b_if[i], a_pool_w[i], a_pool_scale[i], a_conv_w[i], a_conv_b[i],
                      a_head_norm[i], wa, wb)
            zeros = lambda *s: jnp.zeros(s, F32)
            xp, pb, cb, c, n, m = layer_a(
                xp, mp, 0, zeros(bp, state_pool.shape[2], pw), zeros(bp, state_conv.shape[2], state_conv.shape[3]),
                zeros(bp, *state_C.shape[2:]), zeros(bp, *state_n.shape[2:]), zeros(bp, mheads), *shared, tm_p, tt_p)
            xs, pb2, cb2, c2, n2, m2 = layer_a(xs, ms, past, state_pool[i], state_conv[i], state_C[i], state_n[i],
                                                state_m[i], *shared, tm_s, ts)
            for lst, val in zip(outs_a, (pb, pb2, cb, cb2, c, c2, n, n2, m, m2)):
                lst.append(val)
        else:
            mods = ada_mod(c_all, c_ada_w[i], c_ada_b[i])
            mp, ms = [m[:bp] for m in mods], [m[bp:] for m in mods]
            w_in, segs, bg = _layer_c_weights(c_w_in[i], c_b_gate[i], nsa_heads, kvw)
            w_out = c_w_out[i].astype(BF16)
            cmp_w = (c_cmp_k_w1[i], c_cmp_k_b1[i], c_cmp_k_w2[i], c_cmp_v_w1[i], c_cmp_v_b1[i], c_cmp_v_w2[i])
            last = layer == na + nc - 1

            tabs = rope_tables(jnp.arange(tp))
            q, kcr, vcr, ks, vs, kw, vw, z, gt = proj_in_c(xp, mp[0], mp[1], c_norm[i], w_in, segs, tabs, bg, tm_p)
            lohi = cmp_pool_dense(kcr, vcr, c_cmp_k_pos[i], c_cmp_v_pos[i])
            kc, vc = cmp_mlp(*lohi, *cmp_w)
            o = attn_prompt(q, gt, kc, vc, ks, vs, kw, vw, tp // CMP_STRIDE - 1)
            xp = proj_out_c(xp, o, z, mp[2], w_out, final_norm, last, tm_p)
            wbp = min(WINDOW, tp)
            five = lambda a: a.reshape(a.shape[0], a.shape[1], kv_groups, hd)
            vals_p = (five(kcr), five(vcr), five(ks), five(vs), five(kw[:, tp - wbp:]), five(vw[:, tp - wbp:]))

            pos_s = past + jnp.arange(ts)
            tabs = tuple(jnp.tile(a, (bs, 1)) for a in rope_tables(pos_s))
            q, kcr, vcr, ks, vs, kw, vw, z, gt = proj_in_c(xs, ms[0], ms[1], c_norm[i], w_in, segs, tabs, bg, tm_s)
            s_tot = past + ts
            n_sub = s_tot // CMP_STRIDE
            assert n_sub * CMP_STRIDE == past, "compression windows must end inside the paged past"
            flat = lambda a: a.reshape(a.shape[0], a.shape[1], kvw)
            lohi = cmp_pool_paged(page_table, flat(cache_cmp_k[i]), flat(cache_cmp_v[i]), c_cmp_k_pos[i], c_cmp_v_pos[i])
            kc, vc = cmp_mlp(*lohi, *cmp_w)
            n_sel = -(-s_tot // SEL_BLOCK)
            n_past_blk = past // SEL_BLOCK
            assert n_past_blk >= SEL_TOP and page % SEL_BLOCK == 0
            oc, idx = attn_select(q, gt, kc, vc, past, n_sub - 1, n_sel)
            idx = idx[:, :, :ts * kv_groups]
            n_new = (n_sel - n_past_blk) * SEL_BLOCK
            padn = lambda a: jnp.pad(a, ((0, 0), (0, n_new - ts), (0, 0)))
            win_k, win_v = flat(cache_win_k[i]), flat(cache_win_v[i])
            o = attn_gather(idx, page_table, q, gt, oc, win_k, win_v, kw, vw, flat(cache_sel_k[i]), flat(cache_sel_v[i]),
                            padn(ks), padn(vs), past)
            xs = proj_out_c(xs, o, z, ms[2], w_out, final_norm, last, tm_s)
            kw_all = jnp.concatenate([win_k, kw], axis=1)[:, ts:]
            vw_all = jnp.concatenate([win_v, vw], axis=1)[:, ts:]
            vals_s = (five(kcr), five(vcr), five(ks), five(vs), five(kw_all), five(vw_all))
            for j in range(6):
                outs_c[2 * j].append(vals_p[j])
                outs_c[2 * j + 1].append(vals_s[j])
    return (xp, xs) + tuple(jnp.stack(v) for v in outs_a) + tuple(jnp.stack(v) for v in outs_c)
```
